```python
import math
import jax, jax.numpy as jnp
from jax import lax
import numpy as np

D_MODEL = 2048
BATCH = 1
SEQ = 8192
DEPTH = 2
DEC_BATCH = 128
DEC_SEQ = 8
PAST_LEN = 2048
PAGE_SIZE = 128

HEAD_DIM = 64
ATTN_WIDTH = D_MODEL // 2
CONV_WIDTH = D_MODEL - ATTN_WIDTH
N_HEADS_A = ATTN_WIDTH // HEAD_DIM
IN_COLS = 3 * ATTN_WIDTH + 2 * CONV_WIDTH
DILATION_PATTERNS = ((128, 1), (512, 4), (2048, 16))
MAX_WINDOW = 2048
CONV_K = 31
N_BUCKETS = 32
MAX_DISTANCE = 2048
D_FF = 5632
N_EXPERTS = 8
TOP_K = 2
EXPERT_FF = 7168
N_DENSE = (DEPTH + 1) // 2
N_MOE = DEPTH // 2
RMS_EPS = 1e-6
LN_EPS = 1e-5

kernel_name = 'hymba_dilated_conformer_decoder_step'


def rms_norm(x, g):
    xf = x.astype(jnp.float32)
    y = xf * lax.rsqrt(jnp.mean(xf * xf, axis=-1, keepdims=True) + RMS_EPS)
    return (y * g.astype(jnp.float32)).astype(x.dtype)


def layer_norm(x, g, b):
    xf = x.astype(jnp.float32)
    mu = jnp.mean(xf, axis=-1, keepdims=True)
    var = jnp.mean(jnp.square(xf - mu), axis=-1, keepdims=True)
    y = (xf - mu) * lax.rsqrt(var + LN_EPS) * g.astype(jnp.float32) + b.astype(jnp.float32)
    return y.astype(x.dtype)


def t5_bucket(dist):
    max_exact = N_BUCKETS // 2
    d_f = jnp.maximum(dist, 1).astype(jnp.float32)
    large = max_exact + (jnp.log(d_f / max_exact) / math.log(MAX_DISTANCE / max_exact)
                         * (N_BUCKETS - max_exact)).astype(jnp.int32)
    large = jnp.minimum(large, N_BUCKETS - 1)
    return jnp.where(dist < max_exact, dist, large)


def pos_bias(dist, table):
    return table[t5_bucket(dist)].astype(jnp.float32)


def masked_softmax_stats(s, valid):
    s = jnp.where(valid, s, -jnp.inf)
    m = jnp.max(s, axis=-1, keepdims=True)
    p = jnp.exp(s - m)
    den = jnp.sum(p, axis=-1, keepdims=True)
    return p / den, (m + jnp.log(den))[..., 0]


def band_dilated_attention(q, k, v, table, window, dilation):
    B, S, H, Dh = q.shape
    blk = window // dilation
    span = blk * dilation
    sp = -(-S // span) * span
    L = sp // dilation
    nb = L // blk

    def to_blocks(t):
        t = jnp.pad(t.astype(jnp.float32), ((0, 0), (0, sp - S), (0, 0), (0, 0)))
        t = t.reshape(B, L, dilation, H, Dh).transpose(0, 2, 1, 3, 4)
        return t.reshape(B, dilation, nb, blk, H, Dh)

    def with_prev(t):
        prev = jnp.pad(t, ((0, 0), (0, 0), (1, 0), (0, 0), (0, 0), (0, 0)))[:, :, :-1]
        return jnp.concatenate([prev, t], axis=3)

    qb = to_blocks(q)
    kc = with_prev(to_blocks(k))
    vc = with_prev(to_blocks(v))
    qi = jnp.arange(blk)[:, None]
    kj = jnp.arange(2 * blk)[None, :]
    rel = blk + qi - kj
    in_band = (rel >= 0) & (rel <= blk)
    bias = jnp.transpose(pos_bias(jnp.maximum(rel, 0) * dilation, table), (2, 0, 1))
    first = (jnp.arange(nb) == 0)[:, None, None] & (kj < blk)[None]
    valid = (in_band[None] & ~first)[:, None]
    s = jnp.einsum('brnqhd,brnkhd->brnhqk', qb, kc) * (Dh ** -0.5) + bias
    p, lse = masked_softmax_stats(s, valid)
    o = jnp.einsum('brnhqk,brnkhd->brnqhd', p, vc)
    o = o.reshape(B, dilation, L, H, Dh).transpose(0, 2, 1, 3, 4).reshape(B, sp, H, Dh)[:, :S]
    lse = jnp.transpose(lse, (0, 1, 2, 4, 3)).reshape(B, dilation, L, H)
    lse = lse.transpose(0, 2, 1, 3).reshape(B, sp, H)[:, :S]
    return o, lse


def gathered_dilated_attention(q, k_all, v_all, table, window, dilation):
    Bd, T, H, Dh = q.shape
    W = k_all.shape[1] - T
    n = window // dilation
    j = jnp.arange(n + 1)
    idx = W + jnp.arange(T)[:, None] - j[None, :] * dilation
    valid = (idx >= 0)[None, :, None, :]
    idx = jnp.maximum(idx, 0)
    kg = jnp.take(k_all, idx, axis=1).astype(jnp.float32)
    vg = jnp.take(v_all, idx, axis=1).astype(jnp.float32)
    bias = pos_bias(j * dilation, table).T
    s = jnp.einsum('bqhd,bqjhd->bqhj', q.astype(jnp.float32), kg) * (Dh ** -0.5) + bias
    p, lse = masked_softmax_stats(s, valid)
    o = jnp.einsum('bqhj,bqjhd->bqhd', p, vg)
    return o, lse


def hybrid_mixer(h, l, P, past_k, past_v, past_u):
    B, S, _ = h.shape
    proj = h @ P['w_in'][l]
    q, k, v, a, gate = jnp.split(
        proj, [ATTN_WIDTH, 2 * ATTN_WIDTH, 3 * ATTN_WIDTH, 3 * ATTN_WIDTH + CONV_WIDTH], axis=-1)
    q = q.reshape(B, S, N_HEADS_A, HEAD_DIM)
    k = k.reshape(B, S, N_HEADS_A, HEAD_DIM)
    v = v.reshape(B, S, N_HEADS_A, HEAD_DIM)
    table = P['rel_bias_table']
    if past_k is None:
        res = [band_dilated_attention(q, k, v, table, w, d) for (w, d) in DILATION_PATTERNS]
        keep = min(MAX_WINDOW, S)
        new_k, new_v = k[:, S - keep:], v[:, S - keep:]
        u_prev = jnp.zeros((B, CONV_K - 1, CONV_WIDTH), h.dtype)
    else:
        k_all = jnp.concatenate([past_k.astype(k.dtype), k], axis=1)
        v_all = jnp.concatenate([past_v.astype(v.dtype), v], axis=1)
        res = [gathered_dilated_attention(q, k_all, v_all, table, w, d) for (w, d) in DILATION_PATTERNS]
        keep = past_k.shape[1]
        new_k, new_v = k_all[:, -keep:], v_all[:, -keep:]
        u_prev = past_u.astype(h.dtype)
    o = jnp.stack([r[0] for r in res], axis=0)
    lse = jnp.stack([r[1] for r in res], axis=0)
    wts = jax.nn.softmax(lse, axis=0)
    attn = jnp.sum(wts[..., None] * o, axis=0).reshape(B, S, ATTN_WIDTH).astype(h.dtype)
    u = a * jax.nn.sigmoid(gate)
    u_pad = jnp.concatenate([u_prev, u], axis=1)
    new_u = u_pad[:, -(CONV_K - 1):]
    c = lax.conv_general_dilated(u_pad, P['conv_w'][l][:, None, :], (1,), 'VALID',
                                 dimension_numbers=('NWC', 'WIO', 'NWC'),
                                 feature_group_count=CONV_WIDTH) + P['conv_b'][l]
    c = jax.nn.silu(layer_norm(c, P['conv_ln_g'][l], P['conv_ln_b'][l]))
    merged = jnp.concatenate([rms_norm(attn, P['g_attn_out'][l]), rms_norm(c, P['g_conv_out'][l])], axis=-1)
    return merged @ P['w_out'][l], new_k, new_v, new_u


def swiglu(h, wg, wu, wd):
    return (jax.nn.silu(h @ wg) * (h @ wu)) @ wd


def moe_swiglu(h, router_w, wg, wu, wd):
    logits = (h @ router_w).astype(jnp.float32)
    top_v, top_i = lax.top_k(logits, TOP_K)
    gates = jax.nn.softmax(top_v, axis=-1)
    combine = jnp.sum(jax.nn.one_hot(top_i, N_EXPERTS, dtype=jnp.float32) * gates[..., None], axis=-2)
    out = jnp.zeros(h.shape, jnp.float32)
    for e in range(N_EXPERTS):
        out = out + combine[:, e:e + 1] * swiglu(h, wg[e], wu[e], wd[e]).astype(jnp.float32)
    return out.astype(h.dtype)


def decoder_layer(x, l, P, past_k, past_v, past_u):
    B, S, D = x.shape
    y, nk, nv, nu = hybrid_mixer(rms_norm(x, P['g_pre_mix'][l]), l, P, past_k, past_v, past_u)
    x = x + rms_norm(y, P['g_post_mix'][l])
    hf = rms_norm(x, P['g_pre_ffn'][l]).reshape(B * S, D)
    if l % 2 == 0:
        i = l // 2
        f = swiglu(hf, P['dense_w_gate'][i], P['dense_w_up'][i], P['dense_w_down'][i])
    else:
        i = l // 2
        f = moe_swiglu(hf, P['router_w'][i], P['moe_w_gate'][i], P['moe_w_up'][i], P['moe_w_down'][i])
    x = x + rms_norm(f.reshape(B, S, D), P['g_post_ffn'][l])
    return x, nk, nv, nu


def setup_inputs(seed: int = 0) -> dict:
    key = jax.random.key(seed)
    ks = jax.random.split(key, 25)
    f32 = jnp.float32
    win = min(MAX_WINDOW, PAST_LEN)

    def nrm(k, shape, scale):
        return scale * jax.random.normal(k, shape, f32)

    def gain(k, shape):
        return 1.0 + 0.05 * jax.random.normal(k, shape, f32)

    return {
        'x_prompt': nrm(ks[0], (BATCH, SEQ, D_MODEL), 1.0),
        'x_sample': nrm(ks[1], (DEC_BATCH, DEC_SEQ, D_MODEL), 1.0),
        'cache_win_k': nrm(ks[2], (DEPTH, DEC_BATCH, win, N_HEADS_A, HEAD_DIM), 1.0),
        'cache_win_v': nrm(ks[3], (DEPTH, DEC_BATCH, win, N_HEADS_A, HEAD_DIM), 1.0),
        'state_conv': nrm(ks[4], (DEPTH, DEC_BATCH, CONV_K - 1, CONV_WIDTH), 0.5),
        'w_in': nrm(ks[5], (DEPTH, D_MODEL, IN_COLS), D_MODEL ** -0.5),
        'w_out': nrm(ks[6], (DEPTH, ATTN_WIDTH + CONV_WIDTH, D_MODEL), (ATTN_WIDTH + CONV_WIDTH) ** -0.5),
        'rel_bias_table': nrm(ks[7], (N_BUCKETS, N_HEADS_A), 0.3),
        'conv_w': nrm(ks[8], (DEPTH, CONV_K, CONV_WIDTH), CONV_K ** -0.5),
        'conv_b': nrm(ks[9], (DEPTH, CONV_WIDTH), 0.02),
        'conv_ln_g': gain(ks[10], (DEPTH, CONV_WIDTH)),
        'conv_ln_b': nrm(ks[11], (DEPTH, CONV_WIDTH), 0.02),
        'g_attn_out': gain(ks[12], (DEPTH, ATTN_WIDTH)),
        'g_conv_out': gain(ks[13], (DEPTH, CONV_WIDTH)),
        'g_pre_mix': gain(ks[14], (DEPTH, D_MODEL)),
        'g_post_mix': gain(ks[15], (DEPTH, D_MODEL)),
        'g_pre_ffn': gain(ks[16], (DEPTH, D_MODEL)),
        'g_post_ffn': gain(ks[17], (DEPTH, D_MODEL)),
        'dense_w_gate': nrm(ks[18], (N_DENSE, D_MODEL, D_FF), D_MODEL ** -0.5),
        'dense_w_up': nrm(ks[19], (N_DENSE, D_MODEL, D_FF), D_MODEL ** -0.5),
        'dense_w_down': nrm(ks[20], (N_DENSE, D_FF, D_MODEL), D_FF ** -0.5),
        'router_w': nrm(ks[21], (N_MOE, D_MODEL, N_EXPERTS), D_MODEL ** -0.5),
        'moe_w_gate': nrm(ks[22], (N_MOE, N_EXPERTS, D_MODEL, EXPERT_FF), D_MODEL ** -0.5),
        'moe_w_up': nrm(ks[23], (N_MOE, N_EXPERTS, D_MODEL, EXPERT_FF), D_MODEL ** -0.5),
        'moe_w_down': nrm(ks[24], (N_MOE, N_EXPERTS, EXPERT_FF, D_MODEL), EXPERT_FF ** -0.5),
    }


def reference(x_prompt, x_sample, cache_win_k, cache_win_v, state_conv, w_in, w_out, rel_bias_table,
              conv_w, conv_b, conv_ln_g, conv_ln_b, g_attn_out, g_conv_out, g_pre_mix, g_post_mix,
              g_pre_ffn, g_post_ffn, dense_w_gate, dense_w_up, dense_w_down, router_w,
              moe_w_gate, moe_w_up, moe_w_down):
    P = {
        'w_in': w_in, 'w_out': w_out, 'rel_bias_table': rel_bias_table,
        'conv_w': conv_w, 'conv_b': conv_b, 'conv_ln_g': conv_ln_g, 'conv_ln_b': conv_ln_b,
        'g_attn_out': g_attn_out, 'g_conv_out': g_conv_out,
        'g_pre_mix': g_pre_mix, 'g_post_mix': g_post_mix, 'g_pre_ffn': g_pre_ffn, 'g_post_ffn': g_post_ffn,
        'dense_w_gate': dense_w_gate, 'dense_w_up': dense_w_up, 'dense_w_down': dense_w_down,
        'router_w': router_w, 'moe_w_gate': moe_w_gate, 'moe_w_up': moe_w_up, 'moe_w_down': moe_w_down,
    }
    xp, xs = x_prompt, x_sample
    kp, vp, up, ksa, vsa, usa = [], [], [], [], [], []
    for l in range(DEPTH):
        xp, nk, nv, nu = decoder_layer(xp, l, P, None, None, None)
        kp.append(nk); vp.append(nv); up.append(nu)
        xs, nk, nv, nu = decoder_layer(xs, l, P, cache_win_k[l], cache_win_v[l], state_conv[l])
        ksa.append(nk); vsa.append(nv); usa.append(nu)
    return (xp, xs, jnp.stack(kp), jnp.stack(vp), jnp.stack(up),
            jnp.stack(ksa), jnp.stack(vsa), jnp.stack(usa))
```

```python
import functools
import math

import jax
import jax.numpy as jnp
from jax import lax
from jax.experimental import pallas as pl
from jax.experimental.pallas import tpu as pltpu

F32 = jnp.float32
BF16 = jnp.bfloat16

HEAD_DIM = 64
DILATION_PATTERNS = ((128, 1), (512, 4), (2048, 16))
BAND = 128
CONV_K = 31
N_BUCKETS = 32
MAX_DISTANCE = 2048
RMS_EPS = 1e-6
LN_EPS = 1e-5
NEG = -1e30
LANES = 128
SUBLANES = 8
VMEM_LIMIT = 56 * 1024 * 1024


def _cparams(*sem):
    return pltpu.CompilerParams(dimension_semantics=sem, vmem_limit_bytes=VMEM_LIMIT)


def _pick(n, pref, mult=64):
    best = None
    for t in range(mult, min(n, pref) + 1, mult):
        if n % t == 0:
            best = t
    assert best is not None, (n, pref, mult)
    return best


def _sigmoid(x):
    return 1.0 / (1.0 + jnp.exp(-x))


def _rms_rows(src_ref, g_ref, dst_ref, rows, chunk=64):
    def body(i, c):
        r = pl.multiple_of(i * chunk, chunk)
        x = src_ref[pl.ds(r, chunk), :].astype(F32)
        ms = jnp.mean(x * x, axis=-1, keepdims=True)
        dst_ref[pl.ds(r, chunk), :] = (x * lax.rsqrt(ms + RMS_EPS) * g_ref[...]).astype(dst_ref.dtype)
        return c
    lax.fori_loop(0, rows // chunk, body, 0)


def _residual_rms_rows(x_ref, y_ref, g_ref, dst_ref, rows, chunk=64):
    def body(i, c):
        r = pl.multiple_of(i * chunk, chunk)
        y = y_ref[pl.ds(r, chunk), :]
        ms = jnp.mean(y * y, axis=-1, keepdims=True)
        dst_ref[pl.ds(r, chunk), :] = x_ref[pl.ds(r, chunk), :] + y * lax.rsqrt(ms + RMS_EPS) * g_ref[...]
        return c
    lax.fori_loop(0, rows // chunk, body, 0)


def _in_proj_kernel(x_ref, g_ref, w_ref, o_ref, xn_ref, *, tm):
    @pl.when(pl.program_id(1) == 0)
    def _():
        _rms_rows(x_ref, g_ref, xn_ref, tm)
    o_ref[...] = jnp.dot(xn_ref[...], w_ref[...].astype(BF16), preferred_element_type=F32)


def _in_proj(x, g, w_in, layer, tm=1024, tn=512):
    m, d = x.shape
    n = w_in.shape[-1]
    tm = _pick(m, tm)
    return pl.pallas_call(
        functools.partial(_in_proj_kernel, tm=tm),
        grid=(m // tm, n // tn),
        in_specs=[
            pl.BlockSpec((tm, d), lambda i, j: (i, 0)),
            pl.BlockSpec((1, d), lambda i, j: (0, 0)),
            pl.BlockSpec((None, d, tn), lambda i, j: (layer, 0, j)),
        ],
        out_specs=pl.BlockSpec((tm, tn), lambda i, j: (i, j)),
        out_shape=jax.ShapeDtypeStruct((m, n), F32),
        scratch_shapes=[pltpu.VMEM((tm, d), BF16)],
        compiler_params=_cparams("parallel", "arbitrary"),
        name="in_proj",
    )(x, g, w_in)


def _t5_bucket(dist):
    max_exact = N_BUCKETS // 2
    d_f = jnp.maximum(dist, 1).astype(F32)
    large = max_exact + (jnp.log(d_f / max_exact) / math.log(MAX_DISTANCE / max_exact)
                         * (N_BUCKETS - max_exact)).astype(jnp.int32)
    large = jnp.minimum(large, N_BUCKETS - 1)
    return jnp.where(dist < max_exact, dist, large)


def _prompt_bias(table):
    qi = jnp.arange(BAND)[:, None]
    kj = jnp.arange(2 * BAND)[None, :]
    rel = BAND + qi - kj
    in_band = (rel >= 0) & (rel <= BAND)
    out = []
    for _, dil in DILATION_PATTERNS:
        b = table[_t5_bucket(jnp.maximum(rel, 0) * dil)].astype(F32)
        out.append(jnp.where(in_band[None], jnp.transpose(b, (2, 0, 1)), NEG))
    return jnp.stack(out)


def _sample_bias(table, n_new, win):
    h = table.shape[1]
    t = jnp.arange(n_new)[:, None]
    idx = jnp.arange(win + LANES)[None, :]
    dist = win + t - idx
    cnt = jnp.zeros(dist.shape, F32)
    for window, dil in DILATION_PATTERNS:
        cnt = cnt + ((dist >= 0) & (dist % dil == 0) & (dist // dil <= window // dil)).astype(F32)
    cnt = jnp.where(idx < win + n_new, cnt, 0.0)
    b = table[_t5_bucket(jnp.maximum(dist, 0))].astype(F32)
    b = jnp.transpose(b, (2, 0, 1)) + jnp.log(jnp.maximum(cnt, 1.0))[None]
    b = jnp.where(cnt[None] > 0, b, NEG).reshape(h * n_new, win + LANES)
    return b[:, :win], b[:, win:]


def _prompt_attn_kernel(q_ref, kc_ref, kp_ref, vc_ref, vp_ref, bias_ref, o_ref,
                        kk, vv, *stats, sb_rows):
    n_pat = len(DILATION_PATTERNS)
    acc_s, m_s, l_s = stats[:n_pat], stats[n_pat:2 * n_pat], stats[2 * n_pat:]
    sb = pl.program_id(1)
    kk[pl.ds(0, sb_rows), :] = kp_ref[...]
    kk[pl.ds(sb_rows, sb_rows), :] = kc_ref[...]
    vv[pl.ds(0, sb_rows), :] = vp_ref[...]
    vv[pl.ds(sb_rows, sb_rows), :] = vc_ref[...]
    lane = lax.broadcasted_iota(jnp.int32, (BAND, LANES), 1)
    lo = lane < HEAD_DIM
    col = lax.broadcasted_iota(jnp.int32, (BAND, 2 * BAND), 1)
    scale = HEAD_DIM ** -0.5
    n_blocks = sb_rows // BAND

    for p, (_, dil) in enumerate(DILATION_PATTERNS):
        def body(t, carry, p=p, dil=dil):
            r = t % dil
            n = t // dil
            qs = n * (BAND * dil) + r
            ks = sb_rows + qs - BAND * dil
            q2 = q_ref[pl.ds(qs, BAND, stride=dil), :] * scale
            k2 = kk[pl.ds(ks, 2 * BAND, stride=dil), :].astype(BF16)
            v2 = vv[pl.ds(ks, 2 * BAND, stride=dil), :].astype(BF16)
            lim = jnp.where(jnp.logical_and(sb == 0, n == 0), BAND, 0)
            os, ms, ls = [], [], []
            for h in range(2):
                qm = jnp.where(lo if h == 0 else jnp.logical_not(lo), q2, 0.0).astype(BF16)
                s = lax.dot_general(qm, k2, (((1,), (1,)), ((), ())), preferred_element_type=F32)
                s = s + bias_ref[p, h]
                s = jnp.where(col < lim, NEG, s)
                m = jnp.max(s, axis=-1, keepdims=True)
                e = jnp.exp(s - m)
                ls.append(jnp.sum(e, axis=-1, keepdims=True))
                ms.append(m)
                os.append(jnp.dot(e.astype(BF16), v2, preferred_element_type=F32))
            rows = pl.ds(qs, BAND, stride=dil)
            acc_s[p][rows, :] = jnp.where(lo, os[0], os[1])
            m_s[p][rows, :] = jnp.where(lo, ms[0], ms[1])
            l_s[p][rows, :] = jnp.where(lo, ls[0], ls[1])
            return carry
        lax.fori_loop(0, n_blocks, body, 0)

    chunk = 256

    def combine(i, carry):
        r = pl.multiple_of(i * chunk, chunk)
        rows = pl.ds(r, chunk)
        m0, m1, m2 = m_s[0][rows, :], m_s[1][rows, :], m_s[2][rows, :]
        mx = jnp.maximum(jnp.maximum(m0, m1), m2)
        w0, w1, w2 = jnp.exp(m0 - mx), jnp.exp(m1 - mx), jnp.exp(m2 - mx)
        den = w0 * l_s[0][rows, :] + w1 * l_s[1][rows, :] + w2 * l_s[2][rows, :]
        num = w0 * acc_s[0][rows, :] + w1 * acc_s[1][rows, :] + w2 * acc_s[2][rows, :]
        o_ref[rows, :] = num / den
        return carry
    lax.fori_loop(0, sb_rows // chunk, combine, 0)


def _prompt_attn(proj, bias, seq, attn_width):
    sb_rows = BAND * DILATION_PATTERNS[-1][1]
    n_pairs = attn_width // LANES
    blk = (sb_rows, LANES)
    bias = bias.reshape(len(DILATION_PATTERNS), n_pairs, 2, BAND, 2 * BAND)
    return pl.pallas_call(
        functools.partial(_prompt_attn_kernel, sb_rows=sb_rows),
        grid=(n_pairs, seq // sb_rows),
        in_specs=[
            pl.BlockSpec(blk, lambda hp, sb: (sb, hp)),
            pl.BlockSpec(blk, lambda hp, sb: (sb, n_pairs + hp)),
            pl.BlockSpec(blk, lambda hp, sb: (jnp.maximum(sb - 1, 0), n_pairs + hp)),
            pl.BlockSpec(blk, lambda hp, sb: (sb, 2 * n_pairs + hp)),
            pl.BlockSpec(blk, lambda hp, sb: (jnp.maximum(sb - 1, 0), 2 * n_pairs + hp)),
            pl.BlockSpec((len(DILATION_PATTERNS), None, 2, BAND, 2 * BAND), lambda hp, sb: (0, hp, 0, 0, 0)),
        ],
        out_specs=pl.BlockSpec(blk, lambda hp, sb: (sb, hp)),
        out_shape=jax.ShapeDtypeStruct((seq, attn_width), F32),
        scratch_shapes=[
            pltpu.VMEM((2 * sb_rows, LANES), F32),
            pltpu.VMEM((2 * sb_rows, LANES), F32),
        ] + [pltpu.VMEM((sb_rows, LANES), F32)] * (3 * len(DILATION_PATTERNS)),
        compiler_params=_cparams("parallel", "arbitrary"),
        name="prompt_attn",
    )(proj, proj, proj, proj, proj, bias)


def _sample_attn_kernel(qbd_ref, kin_ref, knx_ref, knew_ref, vin_ref, vnx_ref, vnew_ref,
                        bias_ref, bnew_ref, kprev_ref, vprev_ref,
                        o_ref, kout_ref, vout_ref,
                        kn_s, vn_s, m_s, l_s, acc_s, *, n_heads, n_new, chunk_keys):
    del kprev_ref, vprev_ref
    b = pl.program_id(0)
    c = pl.program_id(1)
    last = pl.num_programs(1) - 1
    rows_per_key = n_heads
    new_rows = n_new * rows_per_key
    blk_rows = chunk_keys * rows_per_key

    @pl.when(jnp.logical_and(b == 0, c == 0))
    def _():
        kn_s[...] = jnp.zeros_like(kn_s)
        vn_s[...] = jnp.zeros_like(vn_s)

    @pl.when(c == 0)
    def _():
        m_s[...] = jnp.full_like(m_s, NEG)
        l_s[...] = jnp.zeros_like(l_s)
        acc_s[...] = jnp.zeros_like(acc_s)

    cp = 1024

    def copy(i, carry):
        r = pl.multiple_of(i * cp, cp)
        kout_ref[pl.ds(r, cp), :] = kin_ref[pl.ds(r + new_rows, cp), :]
        vout_ref[pl.ds(r, cp), :] = vin_ref[pl.ds(r + new_rows, cp), :]
        return carry
    n_full = (blk_rows - new_rows) // cp
    lax.fori_loop(0, n_full, copy, 0)
    rem = blk_rows - new_rows - n_full * cp
    if rem:
        kout_ref[pl.ds(n_full * cp, rem), :] = kin_ref[pl.ds(n_full * cp + new_rows, rem), :]
        vout_ref[pl.ds(n_full * cp, rem), :] = vin_ref[pl.ds(n_full * cp + new_rows, rem), :]

    @pl.when(c < last)
    def _():
        kout_ref[pl.ds(blk_rows - new_rows, new_rows), :] = knx_ref[...]
        vout_ref[pl.ds(blk_rows - new_rows, new_rows), :] = vnx_ref[...]

    @pl.when(c == last)
    def _():
        kout_ref[pl.ds(blk_rows - new_rows, new_rows), :] = knew_ref[...]
        vout_ref[pl.ds(blk_rows - new_rows, new_rows), :] = vnew_ref[...]

    def attend(k_ref, v_ref, bias_t, n_keys):
        s = jnp.zeros((n_keys, LANES), F32)
        for h in range(n_heads):
            kh = k_ref[pl.ds(h, n_keys, stride=rows_per_key), :].astype(BF16)
            s = s + jnp.dot(kh, qbd_ref[h], preferred_element_type=F32)
        st = s.T + bias_t
        m_old = m_s[...]
        m_new = jnp.maximum(m_old, jnp.max(st, axis=-1, keepdims=True))
        alpha = jnp.exp(m_old - m_new)
        p = jnp.exp(st - m_new)
        l_s[...] = alpha * l_s[...] + jnp.sum(p, axis=-1, keepdims=True)
        m_s[...] = m_new
        acc_s[...] = acc_s[...] * alpha
        for h in range(n_heads):
            vh = v_ref[pl.ds(h, n_keys, stride=rows_per_key), :].astype(BF16)
            rows = pl.ds(h * n_new, n_new)
            ph = p[h * n_new:(h + 1) * n_new, :].astype(BF16)
            acc_s[rows, :] += jnp.dot(ph, vh, preferred_element_type=F32)

    attend(kin_ref, vin_ref, bias_ref[...], chunk_keys)

    @pl.when(c == last)
    def _():
        kn_s[pl.ds(0, new_rows), :] = knew_ref[...]
        vn_s[pl.ds(0, new_rows), :] = vnew_ref[...]
        attend(kn_s, vn_s, bnew_ref[...], LANES)
        o_ref[...] = acc_s[...] / l_s[...]


def _sample_attn(qbd, knew, vnew, cache_k, cache_v, prev_k, prev_v, bias_c, bias_n, layer,
                 n_heads, n_new, chunk_keys=512):
    depth, batch, rows, _ = cache_k.shape
    win = rows // n_heads
    n_chunks = win // chunk_keys
    blk_rows = chunk_keys * n_heads
    new_rows = n_new * n_heads
    hq = n_heads * n_new
    nx_per_blk = blk_rows // new_rows
    n_nx = rows // new_rows

    cache_spec = pl.BlockSpec((None, None, blk_rows, HEAD_DIM), lambda b, c: (layer, b, c, 0))
    next_spec = pl.BlockSpec((None, None, new_rows, HEAD_DIM),
                             lambda b, c: (layer, b, jnp.minimum((c + 1) * nx_per_blk, n_nx - 1), 0))
    new_spec = pl.BlockSpec((None, new_rows, HEAD_DIM), lambda b, c: (b, 0, 0))
    any_spec = pl.BlockSpec(memory_space=pl.ANY)
    args = [qbd, cache_k, cache_k, knew, cache_v, cache_v, vnew, bias_c, bias_n]
    in_specs = [
        pl.BlockSpec((None, n_heads, HEAD_DIM, LANES), lambda b, c: (b, 0, 0, 0)),
        cache_spec, next_spec, new_spec, cache_spec, next_spec, new_spec,
        pl.BlockSpec((hq, chunk_keys), lambda b, c: (0, c)),
        pl.BlockSpec((hq, LANES), lambda b, c: (0, 0)),
    ]
    aliases = {}
    if prev_k is not None:
        args += [prev_k, prev_v]
        in_specs += [any_spec, any_spec]
        aliases = {len(args) - 2: 1, len(args) - 1: 2}
        body = _sample_attn_kernel
    else:
        def body(*refs, **kw):
            return _sample_attn_kernel(*refs[:9], None, None, *refs[9:], **kw)
    out_cache_spec = pl.BlockSpec((None, None, blk_rows, HEAD_DIM), lambda b, c: (layer, b, c, 0))
    return pl.pallas_call(
        functools.partial(body, n_heads=n_heads, n_new=n_new, chunk_keys=chunk_keys),
        grid=(batch, n_chunks),
        in_specs=in_specs,
        out_specs=[
            pl.BlockSpec((None, hq, HEAD_DIM), lambda b, c: (b, 0, 0)),
            out_cache_spec, out_cache_spec,
        ],
        out_shape=[
            jax.ShapeDtypeStruct((batch, hq, HEAD_DIM), F32),
            jax.ShapeDtypeStruct(cache_k.shape, F32),
            jax.ShapeDtypeStruct(cache_v.shape, F32),
        ],
        scratch_shapes=[
            pltpu.VMEM((LANES * n_heads, HEAD_DIM), F32),
            pltpu.VMEM((LANES * n_heads, HEAD_DIM), F32),
            pltpu.VMEM((hq, 1), F32),
            pltpu.VMEM((hq, 1), F32),
            pltpu.VMEM((hq, HEAD_DIM), F32),
        ],
        input_output_aliases=aliases,
        compiler_params=_cparams("arbitrary", "arbitrary"),
        name="sample_attn",
    )(*args)


def _ln_silu(c, g, b):
    mu = jnp.mean(c, axis=-1, keepdims=True)
    var = jnp.mean(jnp.square(c - mu), axis=-1, keepdims=True)
    y = (c - mu) * lax.rsqrt(var + LN_EPS) * g + b
    return y * _sigmoid(y)


def _conv_prompt_kernel(a_ref, gt_ref, ah_ref, gh_ref, w_ref, cb_ref, lg_ref, lb_ref,
                        c_ref, tail_ref, u_s, *, tm, halo):
    i = pl.program_id(0)
    u_s[pl.ds(0, halo), :] = jnp.where(i == 0, 0.0, ah_ref[...] * _sigmoid(gh_ref[...]))
    rc = 64

    def glu(j, carry):
        r = pl.multiple_of(j * rc, rc)
        u_s[pl.ds(halo + r, rc), :] = a_ref[pl.ds(r, rc), :] * _sigmoid(gt_ref[pl.ds(r, rc), :])
        return carry
    lax.fori_loop(0, tm // rc, glu, 0)
    tail_ref[...] = u_s[pl.ds(tm, halo), :]

    off = halo - (CONV_K - 1)
    for j in range(tm // SUBLANES):
        r = j * SUBLANES
        acc = jnp.zeros((SUBLANES, u_s.shape[1]), F32)
        for k in range(CONV_K):
            acc = acc + u_s[pl.ds(r + off + k, SUBLANES), :] * w_ref[pl.ds(k, 1), :]
        c_ref[pl.ds(r, SUBLANES), :] = _ln_silu(acc + cb_ref[...], lg_ref[...], lb_ref[...])


def _conv_prompt(proj, seq, a_col, g_col, width, w, cb, lg, lb, tm=128):
    halo = 32
    per = tm // halo
    return pl.pallas_call(
        functools.partial(_conv_prompt_kernel, tm=tm, halo=halo),
        grid=(seq // tm,),
        in_specs=[
            pl.BlockSpec((tm, width), lambda i: (i, a_col)),
            pl.BlockSpec((tm, width), lambda i: (i, g_col)),
            pl.BlockSpec((halo, width), lambda i: (jnp.maximum(i * per - 1, 0), a_col)),
            pl.BlockSpec((halo, width), lambda i: (jnp.maximum(i * per - 1, 0), g_col)),
            pl.BlockSpec((halo, width), lambda i: (0, 0)),
            pl.BlockSpec((1, width), lambda i: (0, 0)),
            pl.BlockSpec((1, width), lambda i: (0, 0)),
            pl.BlockSpec((1, width), lambda i: (0, 0)),
        ],
        out_specs=[
            pl.BlockSpec((tm, width), lambda i: (i, 0)),
            pl.BlockSpec((halo, width), lambda i: (0, 0)),
        ],
        out_shape=[
            jax.ShapeDtypeStruct((seq, width), F32),
            jax.ShapeDtypeStruct((halo, width), F32),
        ],
        scratch_shapes=[pltpu.VMEM((tm + halo, width), F32)],
        compiler_params=_cparams("arbitrary"),
        name="conv_prompt",
    )(proj, proj, proj, proj, w, cb, lg, lb)


def _conv_sample_kernel(a_ref, gt_ref, st_ref, w_ref, cb_ref, lg_ref, lb_ref,
                        c_ref, nu_ref, u_s, *, bb, n_new):
    n_state = CONV_K - 1

    def body(b, carry):
        u_s[pl.ds(0, n_state), :] = st_ref[b]
        u_s[pl.ds(n_state, n_new), :] = a_ref[b] * _sigmoid(gt_ref[b])
        acc = jnp.zeros((n_new, u_s.shape[1]), F32)
        for k in range(CONV_K):
            acc = acc + u_s[pl.ds(k, n_new), :] * w_ref[pl.ds(k, 1), :]
        c_ref[b] = _ln_silu(acc + cb_ref[...], lg_ref[...], lb_ref[...])
        nu_ref[b] = u_s[pl.ds(n_new, n_state), :]
        return carry
    lax.fori_loop(0, bb, body, 0)


def _conv_sample(proj_s, state, layer, a_col, g_col, width, w, cb, lg, lb, bb=16):
    batch, n_new, _ = proj_s.shape
    n_state = CONV_K - 1
    bb = _pick(batch, bb, mult=1)
    return pl.pallas_call(
        functools.partial(_conv_sample_kernel, bb=bb, n_new=n_new),
        grid=(batch // bb,),
        in_specs=[
            pl.BlockSpec((bb, n_new, width), lambda i: (i, 0, a_col)),
            pl.BlockSpec((bb, n_new, width), lambda i: (i, 0, g_col)),
            pl.BlockSpec((None, bb, n_state, width), lambda i: (layer, i, 0, 0)),
            pl.BlockSpec((32, width), lambda i: (0, 0)),
            pl.BlockSpec((1, width), lambda i: (0, 0)),
            pl.BlockSpec((1, width), lambda i: (0, 0)),
            pl.BlockSpec((1, width), lambda i: (0, 0)),
        ],
        out_specs=[
            pl.BlockSpec((bb, n_new, width), lambda i: (i, 0, 0)),
            pl.BlockSpec((bb, n_state, width), lambda i: (i, 0, 0)),
        ],
        out_shape=[
            jax.ShapeDtypeStruct((batch, n_new, width), F32),
            jax.ShapeDtypeStruct((batch, n_state, width), F32),
        ],
        scratch_shapes=[pltpu.VMEM((n_state + n_new + 2, width), F32)],
        compiler_params=_cparams("arbitrary"),
        name="conv_sample",
    )(proj_s, proj_s, state, w, cb, lg, lb)


def _out_proj_kernel(at_ref, cv_ref, ga_ref, gc_ref, w_ref, x_ref, gp_ref, o_ref, mrg_s, acc_s, *, tm, tk):
    k = pl.program_id(1)
    half = at_ref.shape[1] // tk

    @pl.when(k == 0)
    def _():
        chunk = 64

        def body(i, c):
            r = pl.multiple_of(i * chunk, chunk)
            for src, g, base in ((at_ref, ga_ref, 0), (cv_ref, gc_ref, half)):
                x = src[pl.ds(r, chunk), :]
                ms = jnp.mean(x * x, axis=-1, keepdims=True)
                y = (x * lax.rsqrt(ms + RMS_EPS) * g[...]).astype(BF16)
                for j in range(half):
                    mrg_s[base + j, pl.ds(r, chunk), :] = y[:, j * tk:(j + 1) * tk]
            return c
        lax.fori_loop(0, tm // chunk, body, 0)
        acc_s[...] = jnp.zeros_like(acc_s)

    acc_s[...] += jnp.dot(mrg_s[k], w_ref[...].astype(BF16), preferred_element_type=F32)

    @pl.when(k == pl.num_programs(1) - 1)
    def _():
        _residual_rms_rows(x_ref, acc_s, gp_ref, o_ref, tm)


def _out_proj(attn, conv, ga, gc, w_out, layer, x, gp, tm=512, tk=512):
    m, half = attn.shape
    d = x.shape[1]
    tm = _pick(m, tm)
    return pl.pallas_call(
        functools.partial(_out_proj_kernel, tm=tm, tk=tk),
        grid=(m // tm, 2 * half // tk),
        in_specs=[
            pl.BlockSpec((tm, half), lambda i, k: (i, 0)),
            pl.BlockSpec((tm, half), lambda i, k: (i, 0)),
            pl.BlockSpec((1, half), lambda i, k: (0, 0)),
            pl.BlockSpec((1, half), lambda i, k: (0, 0)),
            pl.BlockSpec((None, tk, d), lambda i, k: (layer, k, 0)),
            pl.BlockSpec((tm, d), lambda i, k: (i, 0)),
            pl.BlockSpec((1, d), lambda i, k: (0, 0)),
        ],
        out_specs=pl.BlockSpec((tm, d), lambda i, k: (i, 0)),
        out_shape=jax.ShapeDtypeStruct((m, d), F32),
        scratch_shapes=[pltpu.VMEM((2 * half // tk, tm, tk), BF16), pltpu.VMEM((tm, d), F32)],
        compiler_params=_cparams("parallel", "arbitrary"),
        name="out_proj",
    )(attn, conv, ga, gc, w_out, x, gp)


def _swiglu_tile(x, wg_ref, wu_ref, wd_ref):
    g = jnp.dot(x, wg_ref[...].astype(BF16), preferred_element_type=F32)
    u = jnp.dot(x, wu_ref[...].astype(BF16), preferred_element_type=F32)
    h = (g * _sigmoid(g) * u).astype(BF16)
    return jnp.dot(h, wd_ref[...].astype(BF16), preferred_element_type=F32)


def _dense_ffn_kernel(x_ref, gi_ref, wg_ref, wu_ref, wd_ref, go_ref, o_ref, hn_s, acc_s, *, tm):
    f = pl.program_id(1)

    @pl.when(f == 0)
    def _():
        _rms_rows(x_ref, gi_ref, hn_s, tm)
        acc_s[...] = jnp.zeros_like(acc_s)

    acc_s[...] += _swiglu_tile(hn_s[...], wg_ref, wu_ref, wd_ref)

    @pl.when(f == pl.num_programs(1) - 1)
    def _():
        _residual_rms_rows(x_ref, acc_s, go_ref, o_ref, tm)


def _dense_ffn(x, gi, wg, wu, wd, idx, go, tm=512, tf=256):
    m, d = x.shape
    ff = wg.shape[-1]
    tm = _pick(m, tm)
    return pl.pallas_call(
        functools.partial(_dense_ffn_kernel, tm=tm),
        grid=(m // tm, ff // tf),
        in_specs=[
            pl.BlockSpec((tm, d), lambda i, f: (i, 0)),
            pl.BlockSpec((1, d), lambda i, f: (0, 0)),
            pl.BlockSpec((None, d, tf), lambda i, f: (idx, 0, f)),
            pl.BlockSpec((None, d, tf), lambda i, f: (idx, 0, f)),
            pl.BlockSpec((None, tf, d), lambda i, f: (idx, f, 0)),
            pl.BlockSpec((1, d), lambda i, f: (0, 0)),
        ],
        out_specs=pl.BlockSpec((tm, d), lambda i, f: (i, 0)),
        out_shape=jax.ShapeDtypeStruct((m, d), F32),
        scratch_shapes=[pltpu.VMEM((tm, d), BF16), pltpu.VMEM((tm, d), F32)],
        compiler_params=_cparams("parallel", "arbitrary"),
        name="dense_ffn",
    )(x, gi, wg, wu, wd, go)


def _router_kernel(x_ref, g_ref, rw_ref, hn_ref, info_ref, *, tm, n_experts):
    chunk = 64

    def body(i, c):
        r = pl.multiple_of(i * chunk, chunk)
        x = x_ref[pl.ds(r, chunk), :]
        ms = jnp.mean(x * x, axis=-1, keepdims=True)
        hn = x * lax.rsqrt(ms + RMS_EPS) * g_ref[...]
        hn_ref[pl.ds(r, chunk), :] = hn.astype(BF16)
        logits = jnp.dot(hn, rw_ref[...], preferred_element_type=F32, precision=lax.Precision.HIGHEST)
        lane = lax.broadcasted_iota(jnp.int32, logits.shape, 1).astype(F32)
        lg = jnp.where(lane < n_experts, logits, -jnp.inf)
        m1 = jnp.max(lg, axis=-1, keepdims=True)
        i1 = jnp.min(jnp.where(lg == m1, lane, float(LANES)), axis=-1, keepdims=True)
        lg2 = jnp.where(lane == i1, -jnp.inf, lg)
        m2 = jnp.max(lg2, axis=-1, keepdims=True)
        i2 = jnp.min(jnp.where(lg2 == m2, lane, float(LANES)), axis=-1, keepdims=True)
        e = jnp.exp(m2 - m1)
        g1 = 1.0 / (1.0 + e)
        g2 = e / (1.0 + e)
        info = jnp.where(lane == 0, i1, jnp.where(lane == 1, i2, jnp.where(lane == 2, g1, jnp.where(lane == 3, g2, 0.0))))
        info_ref[pl.ds(r, chunk), :] = info
        return c
    lax.fori_loop(0, tm // chunk, body, 0)


def _router(x, g, rw_pad, n_experts, tm=512):
    m, d = x.shape
    tm = _pick(m, tm)
    return pl.pallas_call(
        functools.partial(_router_kernel, tm=tm, n_experts=n_experts),
        grid=(m // tm,),
        in_specs=[
            pl.BlockSpec((tm, d), lambda i: (i, 0)),
            pl.BlockSpec((1, d), lambda i: (0, 0)),
            pl.BlockSpec((d, LANES), lambda i: (0, 0)),
        ],
        out_specs=[
            pl.BlockSpec((tm, d), lambda i: (i, 0)),
            pl.BlockSpec((tm, LANES), lambda i: (i, 0)),
        ],
        out_shape=[
            jax.ShapeDtypeStruct((m, d), BF16),
            jax.ShapeDtypeStruct((m, LANES), F32),
        ],
        compiler_params=_cparams("parallel"),
        name="router",
    )(x, g, rw_pad)


def _moe_kernel(be_ref, nv_ref, x_ref, wg_ref, wu_ref, wd_ref, o_ref):
    i = pl.program_id(0)
    f = pl.program_id(1)

    @pl.when(f == 0)
    def _():
        o_ref[...] = jnp.zeros_like(o_ref)

    @pl.when(i < nv_ref[0])
    def _():
        o_ref[...] += _swiglu_tile(x_ref[...], wg_ref, wu_ref, wd_ref)


def _moe_experts(xs, block_expert, n_valid, wg, wu, wd, idx, tm, tf=256):
    rows, d = xs.shape
    ff = wg.shape[-1]
    nf = ff // tf

    def f_eff(i, f, nv):
        return jnp.where(i < nv[0], f, nf - 1)

    grid_spec = pltpu.PrefetchScalarGridSpec(
        num_scalar_prefetch=2,
        grid=(rows // tm, nf),
        in_specs=[
            pl.BlockSpec((tm, d), lambda i, f, be, nv: (i, 0)),
            pl.BlockSpec((None, None, d, tf), lambda i, f, be, nv: (idx, be[i], 0, f_eff(i, f, nv))),
            pl.BlockSpec((None, None, d, tf), lambda i, f, be, nv: (idx, be[i], 0, f_eff(i, f, nv))),
            pl.BlockSpec((None, None, tf, d), lambda i, f, be, nv: (idx, be[i], f_eff(i, f, nv), 0)),
        ],
        out_specs=pl.BlockSpec((tm, d), lambda i, f, be, nv: (i, 0)),
    )
    return pl.pallas_call(
        _moe_kernel,
        grid_spec=grid_spec,
        out_shape=jax.ShapeDtypeStruct((rows, d), F32),
        compiler_params=_cparams("arbitrary", "arbitrary"),
        name="moe_experts",
    )(block_expert, n_valid, xs, wg, wu, wd)


def _moe_combine_kernel(x_ref, y1_ref, y2_ref, info_ref, g_ref, o_ref, *, tm):
    chunk = 64

    def body(i, c):
        r = pl.multiple_of(i * chunk, chunk)
        rows = pl.ds(r, chunk)
        info = info_ref[rows, :]
        y = info[:, 2:3] * y1_ref[rows, :] + info[:, 3:4] * y2_ref[rows, :]
        ms = jnp.mean(y * y, axis=-1, keepdims=True)
        o_ref[rows, :] = x_ref[rows, :] + y * lax.rsqrt(ms + RMS_EPS) * g_ref[...]
        return c
    lax.fori_loop(0, tm // chunk, body, 0)


def _moe_combine(x, y1, y2, info, g, tm=512):
    m, d = x.shape
    tm = _pick(m, tm)
    row = pl.BlockSpec((tm, d), lambda i: (i, 0))
    return pl.pallas_call(
        functools.partial(_moe_combine_kernel, tm=tm),
        grid=(m // tm,),
        in_specs=[row, row, row, pl.BlockSpec((tm, LANES), lambda i: (i, 0)), pl.BlockSpec((1, d), lambda i: (0, 0))],
        out_specs=row,
        out_shape=jax.ShapeDtypeStruct((m, d), F32),
        compiler_params=_cparams("parallel"),
        name="moe_combine",
    )(x, y1, y2, info, g)


def _moe_ffn(x, gi, router_w, wg, wu, wd, idx, go, tm=1024):
    m, d = x.shape
    n_experts = router_w.shape[-1]
    rw_pad = jnp.zeros((d, LANES), F32).at[:, :n_experts].set(router_w[idx])
    hn, info = _router(x, gi, rw_pad, n_experts)

    expert = jnp.concatenate([info[:, 0], info[:, 1]]).astype(jnp.int32)
    n_pairs = 2 * m
    order = jnp.argsort(expert, stable=True)
    counts = jnp.sum(expert[:, None] == jnp.arange(n_experts)[None, :], axis=0).astype(jnp.int32)
    padded = (counts + tm - 1) // tm * tm
    pad_end = jnp.cumsum(padded)
    start = jnp.cumsum(counts) - counts
    sorted_e = expert[order]
    dest_sorted = (pad_end - padded)[sorted_e] + jnp.arange(n_pairs, dtype=jnp.int32) - start[sorted_e]
    n_blocks = -(-n_pairs // tm) + n_experts
    rows = n_blocks * tm
    src_token = jnp.zeros((rows,), jnp.int32).at[dest_sorted].set((order % m).astype(jnp.int32))
    dest = jnp.zeros((n_pairs,), jnp.int32).at[order].set(dest_sorted)
    n_valid = (pad_end[-1] // tm).astype(jnp.int32)
    blk_start = jnp.minimum(jnp.arange(n_blocks, dtype=jnp.int32), n_valid - 1) * tm
    block_expert = jnp.sum(blk_start[:, None] >= pad_end[None, :], axis=1).astype(jnp.int32)

    xs = jnp.take(hn, src_token, axis=0)
    ys = _moe_experts(xs, block_expert, n_valid.reshape(1), wg, wu, wd, idx, tm)
    y1 = jnp.take(ys, dest[:m], axis=0)
    y2 = jnp.take(ys, dest[m:], axis=0)
    return _moe_combine(x, y1, y2, info, go)


def kernel(x_prompt, x_sample, cache_win_k, cache_win_v, state_conv, w_in, w_out, rel_bias_table, conv_w, conv_b, conv_ln_g, conv_ln_b, g_attn_out, g_conv_out, g_pre_mix, g_post_mix, g_pre_ffn, g_post_ffn, dense_w_gate, dense_w_up, dense_w_down, router_w, moe_w_gate, moe_w_up, moe_w_down):
    _, seq, d = x_prompt.shape
    batch, n_new, _ = x_sample.shape
    depth, _, win, n_heads, _ = cache_win_k.shape
    attn_w = n_heads * HEAD_DIM
    conv_wd = conv_w.shape[-1]
    in_cols = w_in.shape[-1]
    assert in_cols == 3 * attn_w + 2 * conv_wd and attn_w == conv_wd
    assert win == DILATION_PATTERNS[-1][0] and seq % win == 0 and seq >= win
    a_col, g_col = 3 * attn_w // conv_wd, 3 * attn_w // conv_wd + 1
    keep = min(win, seq)
    n_state = CONV_K - 1

    x = jnp.concatenate([x_prompt.reshape(seq, d), x_sample.reshape(batch * n_new, d)], axis=0)
    bias_p = _prompt_bias(rel_bias_table)
    bias_c, bias_n = _sample_bias(rel_bias_table, n_new, win)
    ck = cache_win_k.reshape(depth, batch, win * n_heads, HEAD_DIM)
    cv = cache_win_v.reshape(depth, batch, win * n_heads, HEAD_DIM)
    eye = jnp.eye(n_heads, dtype=F32)
    row = lambda v, l: v[l][None, :]

    kp, vp, up, us = [], [], [], []
    nk_s = nv_s = None
    for l in range(depth):
        proj = _in_proj(x, row(g_pre_mix, l), w_in, l)

        attn_p = _prompt_attn(proj, bias_p, seq, attn_w)
        kp.append(proj[seq - keep:seq, attn_w:2 * attn_w].reshape(1, keep, n_heads, HEAD_DIM))
        vp.append(proj[seq - keep:seq, 2 * attn_w:3 * attn_w].reshape(1, keep, n_heads, HEAD_DIM))
        cw = jnp.zeros((32, conv_wd), F32).at[:CONV_K].set(conv_w[l])
        conv_args = (cw, row(conv_b, l), row(conv_ln_g, l), row(conv_ln_b, l))
        conv_p, tail = _conv_prompt(proj, seq, a_col, g_col, conv_wd, *conv_args)
        up.append(tail[32 - n_state:][None])

        proj_s = proj[seq:].reshape(batch, n_new, in_cols)
        q_s = proj_s[:, :, :attn_w].reshape(batch, n_new, n_heads, HEAD_DIM) * (HEAD_DIM ** -0.5)
        qbd = jnp.einsum('bthd,hg->bhdgt', q_s, eye).reshape(batch, n_heads, HEAD_DIM, n_heads * n_new)
        qbd = qbd.astype(BF16)
        knew = proj_s[:, :, attn_w:2 * attn_w].reshape(batch, n_new * n_heads, HEAD_DIM)
        vnew = proj_s[:, :, 2 * attn_w:3 * attn_w].reshape(batch, n_new * n_heads, HEAD_DIM)
        o_s, nk_s, nv_s = _sample_attn(qbd, knew, vnew, ck, cv, nk_s, nv_s, bias_c, bias_n, l, n_heads, n_new)
        attn_s = o_s.reshape(batch, n_heads, n_new, HEAD_DIM).transpose(0, 2, 1, 3).reshape(batch * n_new, attn_w)
        conv_s, nu = _conv_sample(proj_s, state_conv, l, a_col, g_col, conv_wd, *conv_args)
        us.append(nu)

        attn = jnp.concatenate([attn_p, attn_s], axis=0)
        conv = jnp.concatenate([conv_p, conv_s.reshape(batch * n_new, conv_wd)], axis=0)
        x = _out_proj(attn, conv, row(g_attn_out, l), row(g_conv_out, l), w_out, l, x, row(g_post_mix, l))

        if l % 2 == 0:
            x = _dense_ffn(x, row(g_pre_ffn, l), dense_w_gate, dense_w_up, dense_w_down, l // 2, row(g_post_ffn, l))
        else:
            x = _moe_ffn(x, row(g_pre_ffn, l), router_w, moe_w_gate, moe_w_up, moe_w_down, l // 2, row(g_post_ffn, l))

    shape5 = (depth, batch, win, n_heads, HEAD_DIM)
    return (x[:seq].reshape(1, seq, d), x[seq:].reshape(batch, n_new, d),
            jnp.stack(kp), jnp.stack(vp), jnp.stack(up),
            nk_s.reshape(shape5), nv_s.reshape(shape5), jnp.stack(us))
```

```python
import functools
import math

import jax
import jax.numpy as jnp
from jax import lax
from jax.experimental import pallas as pl
from jax.experimental.pallas import tpu as pltpu

F32 = jnp.float32
BF16 = jnp.bfloat16

HEAD_DIM = 64
DILATION_PATTERNS = ((128, 1), (512, 4), (2048, 16))
BAND = 128
CONV_K = 31
N_BUCKETS = 32
MAX_DISTANCE = 2048
RMS_EPS = 1e-6
LN_EPS = 1e-5
NEG = -1e30
LANES = 128
SUBLANES = 8
VMEM_LIMIT = 56 * 1024 * 1024


def _cparams(*sem):
    return pltpu.CompilerParams(dimension_semantics=sem, vmem_limit_bytes=VMEM_LIMIT)


def _pick(n, pref, mult=64):
    best = None
    for t in range(mult, min(n, pref) + 1, mult):
        if n % t == 0:
            best = t
    assert best is not None, (n, pref, mult)
    return best


def _sigmoid(x):
    return 1.0 / (1.0 + jnp.exp(-x))


def _rms_rows(src_ref, g_ref, dst_ref, rows, chunk=64):
    def body(i, c):
        r = pl.multiple_of(i * chunk, chunk)
        x = src_ref[pl.ds(r, chunk), :].astype(F32)
        ms = jnp.mean(x * x, axis=-1, keepdims=True)
        dst_ref[pl.ds(r, chunk), :] = (x * lax.rsqrt(ms + RMS_EPS) * g_ref[...]).astype(dst_ref.dtype)
        return c
    lax.fori_loop(0, rows // chunk, body, 0)


def _residual_rms_rows(x_ref, y_ref, g_ref, dst_ref, rows, chunk=64):
    def body(i, c):
        r = pl.multiple_of(i * chunk, chunk)
        y = y_ref[pl.ds(r, chunk), :]
        ms = jnp.mean(y * y, axis=-1, keepdims=True)
        dst_ref[pl.ds(r, chunk), :] = x_ref[pl.ds(r, chunk), :] + y * lax.rsqrt(ms + RMS_EPS) * g_ref[...]
        return c
    lax.fori_loop(0, rows // chunk, body, 0)


def _in_proj_kernel(x_ref, g_ref, w_ref, o_ref, xn_ref, *, tm):
    @pl.when(pl.program_id(1) == 0)
    def _():
        _rms_rows(x_ref, g_ref, xn_ref, tm)
    o_ref[...] = jnp.dot(xn_ref[...], w_ref[...].astype(BF16), preferred_element_type=F32)


def _in_proj(x, g, w_in, layer, tm=1024, tn=512):
    m, d = x.shape
    n = w_in.shape[-1]
    tm = _pick(m, tm)
    return pl.pallas_call(
        functools.partial(_in_proj_kernel, tm=tm),
        grid=(m // tm, n // tn),
        in_specs=[
            pl.BlockSpec((tm, d), lambda i, j: (i, 0)),
            pl.BlockSpec((1, d), lambda i, j: (0, 0)),
            pl.BlockSpec((None, d, tn), lambda i, j: (layer, 0, j)),
        ],
        out_specs=pl.BlockSpec((tm, tn), lambda i, j: (i, j)),
        out_shape=jax.ShapeDtypeStruct((m, n), F32),
        scratch_shapes=[pltpu.VMEM((tm, d), BF16)],
        compiler_params=_cparams("parallel", "arbitrary"),
        name="in_proj",
    )(x, g, w_in)


def _t5_bucket(dist):
    max_exact = N_BUCKETS // 2
    d_f = jnp.maximum(dist, 1).astype(F32)
    large = max_exact + (jnp.log(d_f / max_exact) / math.log(MAX_DISTANCE / max_exact)
                         * (N_BUCKETS - max_exact)).astype(jnp.int32)
    large = jnp.minimum(large, N_BUCKETS - 1)
    return jnp.where(dist < max_exact, dist, large)


def _toeplitz(f, n_rows, n_cols, first):
    return jax.vmap(lambda r: lax.dynamic_slice_in_dim(f, first - r, n_cols, axis=1))(jnp.arange(n_rows))


def _prompt_bias(table):
    c = jnp.arange(-(BAND - 1), 2 * BAND)
    rel = BAND - c
    in_band = (rel >= 0) & (rel <= BAND)
    out = []
    for _, dil in DILATION_PATTERNS:
        f = table[_t5_bucket(jnp.maximum(rel, 0) * dil)].astype(F32).T
        f = jnp.where(in_band[None], f, NEG)
        out.append(jnp.transpose(_toeplitz(f, BAND, 2 * BAND, BAND - 1), (1, 0, 2)))
    return jnp.stack(out)


def _sample_bias(table, n_new, win):
    h = table.shape[1]
    n = win + LANES + n_new - 1
    dist = win + n_new - 1 - jnp.arange(n)
    cnt = jnp.zeros(dist.shape, F32)
    for window, dil in DILATION_PATTERNS:
        cnt = cnt + ((dist >= 0) & (dist % dil == 0) & (dist // dil <= window // dil)).astype(F32)
    g = table[_t5_bucket(jnp.maximum(dist, 0))].astype(F32).T + jnp.log(jnp.maximum(cnt, 1.0))[None]
    g = jnp.where(cnt[None] > 0, g, NEG)
    b = jnp.transpose(_toeplitz(g, n_new, win + LANES, n_new - 1), (1, 0, 2))
    b = b.reshape(h * n_new, win + LANES)
    bias_old = jnp.where(jnp.arange(LANES)[None, :] < n_new, b[:, :LANES], NEG)
    return b[:, n_new:win + n_new], bias_old


def _prompt_attn_kernel(q_ref, kc_ref, kp_ref, vc_ref, vp_ref, bias_ref, o_ref,
                        kk, vv, *stats, sb_rows):
    n_pat = len(DILATION_PATTERNS)
    acc_s, m_s, l_s = stats[:n_pat], stats[n_pat:2 * n_pat], stats[2 * n_pat:]
    sb = pl.program_id(1)
    kk[pl.ds(0, sb_rows), :] = kp_ref[...]
    kk[pl.ds(sb_rows, sb_rows), :] = kc_ref[...]
    vv[pl.ds(0, sb_rows), :] = vp_ref[...]
    vv[pl.ds(sb_rows, sb_rows), :] = vc_ref[...]
    lane = lax.broadcasted_iota(jnp.int32, (BAND, LANES), 1)
    lo = lane < HEAD_DIM
    col = lax.broadcasted_iota(jnp.int32, (BAND, 2 * BAND), 1)
    scale = HEAD_DIM ** -0.5
    n_blocks = sb_rows // BAND

    for p, (_, dil) in enumerate(DILATION_PATTERNS):
        def body(t, carry, p=p, dil=dil):
            r = t % dil
            n = t // dil
            qs = n * (BAND * dil) + r
            ks = sb_rows + qs - BAND * dil
            q2 = q_ref[pl.ds(qs, BAND, stride=dil), :] * scale
            k2 = kk[pl.ds(ks, 2 * BAND, stride=dil), :].astype(BF16)
            v2 = vv[pl.ds(ks, 2 * BAND, stride=dil), :].astype(BF16)
            lim = jnp.where(jnp.logical_and(sb == 0, n == 0), BAND, 0)
            os, ms, ls = [], [], []
            for h in range(2):
                qm = jnp.where(lo if h == 0 else jnp.logical_not(lo), q2, 0.0).astype(BF16)
                s = lax.dot_general(qm, k2, (((1,), (1,)), ((), ())), preferred_element_type=F32)
                s = s + bias_ref[p, h]
                s = jnp.where(col < lim, NEG, s)
                m = jnp.max(s, axis=-1, keepdims=True)
                e = jnp.exp(s - m)
                ls.append(jnp.sum(e, axis=-1, keepdims=True))
                ms.append(m)
                os.append(jnp.dot(e.astype(BF16), v2, preferred_element_type=F32))
            rows = pl.ds(qs, BAND, stride=dil)
            acc_s[p][rows, :] = jnp.where(lo, os[0], os[1])
            m_s[p][rows, :] = jnp.where(lo, ms[0], ms[1])
            l_s[p][rows, :] = jnp.where(lo, ls[0], ls[1])
            return carry
        lax.fori_loop(0, n_blocks, body, 0, unroll=4)

    chunk = 256

    def combine(i, carry):
        r = pl.multiple_of(i * chunk, chunk)
        rows = pl.ds(r, chunk)
        m0, m1, m2 = m_s[0][rows, :], m_s[1][rows, :], m_s[2][rows, :]
        mx = jnp.maximum(jnp.maximum(m0, m1), m2)
        w0, w1, w2 = jnp.exp(m0 - mx), jnp.exp(m1 - mx), jnp.exp(m2 - mx)
        den = w0 * l_s[0][rows, :] + w1 * l_s[1][rows, :] + w2 * l_s[2][rows, :]
        num = w0 * acc_s[0][rows, :] + w1 * acc_s[1][rows, :] + w2 * acc_s[2][rows, :]
        o_ref[rows, :] = num / den
        return carry
    lax.fori_loop(0, sb_rows // chunk, combine, 0)


def _prompt_attn(proj, bias, seq, attn_width):
    sb_rows = BAND * DILATION_PATTERNS[-1][1]
    n_pairs = attn_width // LANES
    blk = (sb_rows, LANES)
    bias = bias.reshape(len(DILATION_PATTERNS), n_pairs, 2, BAND, 2 * BAND)
    return pl.pallas_call(
        functools.partial(_prompt_attn_kernel, sb_rows=sb_rows),
        grid=(n_pairs, seq // sb_rows),
        in_specs=[
            pl.BlockSpec(blk, lambda hp, sb: (sb, hp)),
            pl.BlockSpec(blk, lambda hp, sb: (sb, n_pairs + hp)),
            pl.BlockSpec(blk, lambda hp, sb: (jnp.maximum(sb - 1, 0), n_pairs + hp)),
            pl.BlockSpec(blk, lambda hp, sb: (sb, 2 * n_pairs + hp)),
            pl.BlockSpec(blk, lambda hp, sb: (jnp.maximum(sb - 1, 0), 2 * n_pairs + hp)),
            pl.BlockSpec((len(DILATION_PATTERNS), None, 2, BAND, 2 * BAND), lambda hp, sb: (0, hp, 0, 0, 0)),
        ],
        out_specs=pl.BlockSpec(blk, lambda hp, sb: (sb, hp)),
        out_shape=jax.ShapeDtypeStruct((seq, attn_width), F32),
        scratch_shapes=[
            pltpu.VMEM((2 * sb_rows, LANES), F32),
            pltpu.VMEM((2 * sb_rows, LANES), F32),
        ] + [pltpu.VMEM((sb_rows, LANES), F32)] * (3 * len(DILATION_PATTERNS)),
        compiler_params=_cparams("parallel", "arbitrary"),
        name="prompt_attn",
    )(proj, proj, proj, proj, proj, bias)


def _sample_attn_kernel(*refs, n_new, heads_per_step, aliased):
    n_in = 9 if aliased else 7
    qbd_ref, kin_ref, vin_ref, knew_ref, vnew_ref, bias_ref, bold_ref = refs[:7]
    o_ref, kout_ref, vout_ref, kb_s, vb_s = refs[n_in:]
    win = kin_ref.shape[1]
    lane = lax.broadcasted_iota(jnp.int32, (HEAD_DIM, LANES), 1)
    tail = win - LANES

    for src, new, dst, dst_b in ((kin_ref, knew_ref, kout_ref, kb_s), (vin_ref, vnew_ref, vout_ref, vb_s)):
        for h in range(heads_per_step):
            rows = pl.ds(h * HEAD_DIM, HEAD_DIM)
            rolled = pltpu.roll(src[rows, :], win - n_new, axis=1)
            last = jnp.where(lane >= LANES - n_new, new[rows, :], rolled[:, tail:])
            dst[rows, pl.ds(0, tail)] = rolled[:, :tail]
            dst[rows, pl.ds(tail, LANES)] = last
            dst_b[rows, pl.ds(0, tail)] = rolled[:, :tail].astype(BF16)
            dst_b[rows, pl.ds(tail, LANES)] = last.astype(BF16)

    qbd = qbd_ref[...]
    s = jnp.dot(qbd, kb_s[...], preferred_element_type=F32) + bias_ref[...]
    s_old = jnp.dot(qbd, kin_ref[:, pl.ds(0, LANES)].astype(BF16), preferred_element_type=F32) + bold_ref[...]
    m = jnp.maximum(jnp.max(s, axis=-1, keepdims=True), jnp.max(s_old, axis=-1, keepdims=True))
    p = jnp.exp(s - m)
    p_old = jnp.exp(s_old - m)
    r = 1.0 / (jnp.sum(p, axis=-1, keepdims=True) + jnp.sum(p_old, axis=-1, keepdims=True))
    nt = (((1,), (1,)), ((), ()))
    ot = lax.dot_general(vb_s[...], (p * r).astype(BF16), nt, preferred_element_type=F32)
    ot = ot + lax.dot_general(vin_ref[:, pl.ds(0, LANES)].astype(BF16), (p_old * r).astype(BF16), nt,
                              preferred_element_type=F32)
    for h in range(heads_per_step):
        o_ref[pl.ds(h * HEAD_DIM, HEAD_DIM), :] = ot[h * HEAD_DIM:(h + 1) * HEAD_DIM, h * n_new:(h + 1) * n_new]


def _sample_attn(qbd, knew_t, vnew_t, cache_k, cache_v, prev_k, prev_v, bias_w, bias_old, layer,
                 n_new, heads_per_step=8):
    depth, batch, width, win = cache_k.shape
    rows = heads_per_step * HEAD_DIM
    groups = width // rows
    hq = heads_per_step * n_new
    cache_spec = pl.BlockSpec((None, None, rows, win), lambda b, g: (layer, b, g, 0))
    new_spec = pl.BlockSpec((None, rows, LANES), lambda b, g: (b, g, 0))
    args = [qbd, cache_k, cache_v, knew_t, vnew_t, bias_w, bias_old]
    in_specs = [
        pl.BlockSpec((None, None, hq, rows), lambda b, g: (b, g, 0, 0)),
        cache_spec, cache_spec, new_spec, new_spec,
        pl.BlockSpec((hq, win), lambda b, g: (g, 0)),
        pl.BlockSpec((hq, LANES), lambda b, g: (g, 0)),
    ]
    aliases = {}
    if prev_k is not None:
        args += [prev_k, prev_v]
        in_specs += [pl.BlockSpec(memory_space=pl.ANY)] * 2
        aliases = {7: 1, 8: 2}
    return pl.pallas_call(
        functools.partial(_sample_attn_kernel, n_new=n_new, heads_per_step=heads_per_step,
                          aliased=prev_k is not None),
        grid=(batch, groups),
        in_specs=in_specs,
        out_specs=[pl.BlockSpec((None, rows, n_new), lambda b, g: (b, g, 0)), cache_spec, cache_spec],
        out_shape=[
            jax.ShapeDtypeStruct((batch, width, n_new), F32),
            jax.ShapeDtypeStruct(cache_k.shape, F32),
            jax.ShapeDtypeStruct(cache_v.shape, F32),
        ],
        scratch_shapes=[pltpu.VMEM((rows, win), BF16), pltpu.VMEM((rows, win), BF16)],
        input_output_aliases=aliases,
        compiler_params=_cparams("parallel", "parallel"),
        name="sample_attn",
    )(*args)


def _ln_silu(c, g, b):
    mu = jnp.mean(c, axis=-1, keepdims=True)
    var = jnp.mean(jnp.square(c - mu), axis=-1, keepdims=True)
    y = (c - mu) * lax.rsqrt(var + LN_EPS) * g + b
    return y * _sigmoid(y)


def _conv_prompt_kernel(a_ref, gt_ref, ah_ref, gh_ref, w_ref, cb_ref, lg_ref, lb_ref,
                        c_ref, tail_ref, u_s, sh_s, *, tm, halo):
    i = pl.program_id(0)
    u_s[pl.ds(0, halo), :] = jnp.where(i == 0, 0.0, ah_ref[...] * _sigmoid(gh_ref[...]))
    rc = 64

    def glu(j, carry):
        r = pl.multiple_of(j * rc, rc)
        u_s[pl.ds(halo + r, rc), :] = a_ref[pl.ds(r, rc), :] * _sigmoid(gt_ref[pl.ds(r, rc), :])
        return carry
    lax.fori_loop(0, tm // rc, glu, 0)
    tail_ref[...] = u_s[pl.ds(tm, halo), :]

    off = halo - (CONV_K - 1)
    n_sh = tm + halo - SUBLANES
    for s in range(1, SUBLANES):
        for r in range(0, n_sh, rc):
            n = min(rc, n_sh - r)
            sh_s[s - 1, pl.ds(r, n), :] = u_s[pl.ds(r + s, n), :]

    def tile(j, carry):
        r = pl.multiple_of(j * 2 * SUBLANES, 2 * SUBLANES)
        acc0 = jnp.zeros((SUBLANES, u_s.shape[1]), F32)
        acc1 = jnp.zeros((SUBLANES, u_s.shape[1]), F32)
        for k in range(CONV_K):
            s, q = (off + k) % SUBLANES, (off + k) // SUBLANES
            w8 = w_ref[pl.ds(k * SUBLANES, SUBLANES), :]
            base = r + q * SUBLANES
            if s == 0:
                u0, u1 = u_s[pl.ds(base, SUBLANES), :], u_s[pl.ds(base + SUBLANES, SUBLANES), :]
            else:
                u0, u1 = sh_s[s - 1, pl.ds(base, SUBLANES), :], sh_s[s - 1, pl.ds(base + SUBLANES, SUBLANES), :]
            acc0 = acc0 + u0 * w8
            acc1 = acc1 + u1 * w8
        c_ref[pl.ds(r, SUBLANES), :] = acc0
        c_ref[pl.ds(r + SUBLANES, SUBLANES), :] = acc1
        return carry
    lax.fori_loop(0, tm // (2 * SUBLANES), tile, 0)

    def norm(j, carry):
        r = pl.multiple_of(j * rc, rc)
        c_ref[pl.ds(r, rc), :] = _ln_silu(c_ref[pl.ds(r, rc), :] + cb_ref[...], lg_ref[...], lb_ref[...])
        return carry
    lax.fori_loop(0, tm // rc, norm, 0)


def _conv_prompt(proj, seq, a_col, g_col, width, w8, cb, lg, lb, tm=512):
    halo = 32
    per = tm // halo
    return pl.pallas_call(
        functools.partial(_conv_prompt_kernel, tm=tm, halo=halo),
        grid=(seq // tm,),
        in_specs=[
            pl.BlockSpec((tm, width), lambda i: (i, a_col)),
            pl.BlockSpec((tm, width), lambda i: (i, g_col)),
            pl.BlockSpec((halo, width), lambda i: (jnp.maximum(i * per - 1, 0), a_col)),
            pl.BlockSpec((halo, width), lambda i: (jnp.maximum(i * per - 1, 0), g_col)),
            pl.BlockSpec((CONV_K * SUBLANES, width), lambda i: (0, 0)),
            pl.BlockSpec((1, width), lambda i: (0, 0)),
            pl.BlockSpec((1, width), lambda i: (0, 0)),
            pl.BlockSpec((1, width), lambda i: (0, 0)),
        ],
        out_specs=[
            pl.BlockSpec((tm, width), lambda i: (i, 0)),
            pl.BlockSpec((halo, width), lambda i: (0, 0)),
        ],
        out_shape=[
            jax.ShapeDtypeStruct((seq, width), F32),
            jax.ShapeDtypeStruct((halo, width), F32),
        ],
        scratch_shapes=[pltpu.VMEM((tm + halo, width), F32), pltpu.VMEM((SUBLANES - 1, tm + halo, width), F32)],
        compiler_params=_cparams("arbitrary"),
        name="conv_prompt",
    )(proj, proj, proj, proj, w8, cb, lg, lb)


def _conv_sample_kernel(a_ref, gt_ref, st_ref, w_ref, cb_ref, lg_ref, lb_ref,
                        c_ref, nu_ref, u_s, *, bb, n_new):
    n_state = CONV_K - 1

    def body(b, carry):
        u_s[pl.ds(0, n_state), :] = st_ref[b]
        u_s[pl.ds(n_state, n_new), :] = a_ref[b] * _sigmoid(gt_ref[b])
        acc = jnp.zeros((n_new, u_s.shape[1]), F32)
        for k in range(CONV_K):
            acc = acc + u_s[pl.ds(k, n_new), :] * w_ref[pl.ds(k, 1), :]
        c_ref[b] = _ln_silu(acc + cb_ref[...], lg_ref[...], lb_ref[...])
        nu_ref[b] = u_s[pl.ds(n_new, n_state), :]
        return carry
    lax.fori_loop(0, bb, body, 0)


def _conv_sample(proj_s, state, layer, a_col, g_col, width, w, cb, lg, lb, bb=16):
    batch, n_new, _ = proj_s.shape
    n_state = CONV_K - 1
    bb = _pick(batch, bb, mult=1)
    return pl.pallas_call(
        functools.partial(_conv_sample_kernel, bb=bb, n_new=n_new),
        grid=(batch // bb,),
        in_specs=[
            pl.BlockSpec((bb, n_new, width), lambda i: (i, 0, a_col)),
            pl.BlockSpec((bb, n_new, width), lambda i: (i, 0, g_col)),
            pl.BlockSpec((None, bb, n_state, width), lambda i: (layer, i, 0, 0)),
            pl.BlockSpec((32, width), lambda i: (0, 0)),
            pl.BlockSpec((1, width), lambda i: (0, 0)),
            pl.BlockSpec((1, width), lambda i: (0, 0)),
            pl.BlockSpec((1, width), lambda i: (0, 0)),
        ],
        out_specs=[
            pl.BlockSpec((bb, n_new, width), lambda i: (i, 0, 0)),
            pl.BlockSpec((bb, n_state, width), lambda i: (i, 0, 0)),
        ],
        out_shape=[
            jax.ShapeDtypeStruct((batch, n_new, width), F32),
            jax.ShapeDtypeStruct((batch, n_state, width), F32),
        ],
        scratch_shapes=[pltpu.VMEM((n_state + n_new + 2, width), F32)],
        compiler_params=_cparams("arbitrary"),
        name="conv_sample",
    )(proj_s, proj_s, state, w, cb, lg, lb)


def _out_proj_kernel(at_ref, cv_ref, ga_ref, gc_ref, w_ref, x_ref, gp_ref, o_ref, mrg_s, acc_s, *, tm, tk):
    k = pl.program_id(1)
    half = at_ref.shape[1] // tk

    @pl.when(k == 0)
    def _():
        chunk = 64

        def body(i, c):
            r = pl.multiple_of(i * chunk, chunk)
            for src, g, base in ((at_ref, ga_ref, 0), (cv_ref, gc_ref, half)):
                x = src[pl.ds(r, chunk), :]
                ms = jnp.mean(x * x, axis=-1, keepdims=True)
                y = (x * lax.rsqrt(ms + RMS_EPS) * g[...]).astype(BF16)
                for j in range(half):
                    mrg_s[base + j, pl.ds(r, chunk), :] = y[:, j * tk:(j + 1) * tk]
            return c
        lax.fori_loop(0, tm // chunk, body, 0)
        acc_s[...] = jnp.zeros_like(acc_s)

    acc_s[...] += jnp.dot(mrg_s[k], w_ref[...].astype(BF16), preferred_element_type=F32)

    @pl.when(k == pl.num_programs(1) - 1)
    def _():
        _residual_rms_rows(x_ref, acc_s, gp_ref, o_ref, tm)


def _out_proj(attn, conv, ga, gc, w_out, layer, x, gp, tm=512, tk=512):
    m, half = attn.shape
    d = x.shape[1]
    tm = _pick(m, tm)
    return pl.pallas_call(
        functools.partial(_out_proj_kernel, tm=tm, tk=tk),
        grid=(m // tm, 2 * half // tk),
        in_specs=[
            pl.BlockSpec((tm, half), lambda i, k: (i, 0)),
            pl.BlockSpec((tm, half), lambda i, k: (i, 0)),
            pl.BlockSpec((1, half), lambda i, k: (0, 0)),
            pl.BlockSpec((1, half), lambda i, k: (0, 0)),
            pl.BlockSpec((None, tk, d), lambda i, k: (layer, k, 0)),
            pl.BlockSpec((tm, d), lambda i, k: (i, 0)),
            pl.BlockSpec((1, d), lambda i, k: (0, 0)),
        ],
        out_specs=pl.BlockSpec((tm, d), lambda i, k: (i, 0)),
        out_shape=jax.ShapeDtypeStruct((m, d), F32),
        scratch_shapes=[pltpu.VMEM((2 * half // tk, tm, tk), BF16), pltpu.VMEM((tm, d), F32)],
        compiler_params=_cparams("parallel", "arbitrary"),
        name="out_proj",
    )(attn, conv, ga, gc, w_out, x, gp)


def _swiglu_tile(x, wg_ref, wu_ref, wd_ref):
    g = jnp.dot(x, wg_ref[...].astype(BF16), preferred_element_type=F32)
    u = jnp.dot(x, wu_ref[...].astype(BF16), preferred_element_type=F32)
    h = (g * _sigmoid(g) * u).astype(BF16)
    return jnp.dot(h, wd_ref[...].astype(BF16), preferred_element_type=F32)


def _dense_ffn_kernel(x_ref, gi_ref, wg_ref, wu_ref, wd_ref, go_ref, o_ref, hn_s, *, tm):
    f = pl.program_id(1)

    @pl.when(f == 0)
    def _():
        _rms_rows(x_ref, gi_ref, hn_s, tm)
        o_ref[...] = jnp.zeros_like(o_ref)

    o_ref[...] += _swiglu_tile(hn_s[...], wg_ref, wu_ref, wd_ref)

    @pl.when(f == pl.num_programs(1) - 1)
    def _():
        _residual_rms_rows(x_ref, o_ref, go_ref, o_ref, tm)


def _dense_ffn(x, gi, wg, wu, wd, idx, go, tm=768, tf=256):
    m, d = x.shape
    ff = wg.shape[-1]
    tm = _pick(m, tm)
    return pl.pallas_call(
        functools.partial(_dense_ffn_kernel, tm=tm),
        grid=(m // tm, ff // tf),
        in_specs=[
            pl.BlockSpec((tm, d), lambda i, f: (i, 0)),
            pl.BlockSpec((1, d), lambda i, f: (0, 0)),
            pl.BlockSpec((None, d, tf), lambda i, f: (idx, 0, f)),
            pl.BlockSpec((None, d, tf), lambda i, f: (idx, 0, f)),
            pl.BlockSpec((None, tf, d), lambda i, f: (idx, f, 0)),
            pl.BlockSpec((1, d), lambda i, f: (0, 0)),
        ],
        out_specs=pl.BlockSpec((tm, d), lambda i, f: (i, 0)),
        out_shape=jax.ShapeDtypeStruct((m, d), F32),
        scratch_shapes=[pltpu.VMEM((tm, d), BF16)],
        compiler_params=_cparams("parallel", "arbitrary"),
        name="dense_ffn",
    )(x, gi, wg, wu, wd, go)


def _router_kernel(x_ref, g_ref, rw_ref, hn_ref, info_ref, *, tm, n_experts):
    chunk = 64

    def body(i, c):
        r = pl.multiple_of(i * chunk, chunk)
        x = x_ref[pl.ds(r, chunk), :]
        ms = jnp.mean(x * x, axis=-1, keepdims=True)
        hn = x * lax.rsqrt(ms + RMS_EPS) * g_ref[...]
        hn_ref[pl.ds(r, chunk), :] = hn
        logits = jnp.dot(hn, rw_ref[...], preferred_element_type=F32, precision=lax.Precision.HIGHEST)
        lane = lax.broadcasted_iota(jnp.int32, logits.shape, 1).astype(F32)
        lg = jnp.where(lane < n_experts, logits, -jnp.inf)
        m1 = jnp.max(lg, axis=-1, keepdims=True)
        i1 = jnp.min(jnp.where(lg == m1, lane, float(LANES)), axis=-1, keepdims=True)
        lg2 = jnp.where(lane == i1, -jnp.inf, lg)
        m2 = jnp.max(lg2, axis=-1, keepdims=True)
        i2 = jnp.min(jnp.where(lg2 == m2, lane, float(LANES)), axis=-1, keepdims=True)
        e = jnp.exp(m2 - m1)
        g1 = 1.0 / (1.0 + e)
        g2 = e / (1.0 + e)
        info = jnp.where(lane == 0, i1, jnp.where(lane == 1, i2, jnp.where(lane == 2, g1, jnp.where(lane == 3, g2, 0.0))))
        info_ref[pl.ds(r, chunk), :] = info
        return c
    lax.fori_loop(0, tm // chunk, body, 0)


def _router(x, g, rw_pad, n_experts, tm=512):
    m, d = x.shape
    tm = _pick(m, tm)
    return pl.pallas_call(
        functools.partial(_router_kernel, tm=tm, n_experts=n_experts),
        grid=(m // tm,),
        in_specs=[
            pl.BlockSpec((tm, d), lambda i: (i, 0)),
            pl.BlockSpec((1, d), lambda i: (0, 0)),
            pl.BlockSpec((d, LANES), lambda i: (0, 0)),
        ],
        out_specs=[
            pl.BlockSpec((tm, d), lambda i: (i, 0)),
            pl.BlockSpec((tm, LANES), lambda i: (i, 0)),
        ],
        out_shape=[
            jax.ShapeDtypeStruct((m, d), F32),
            jax.ShapeDtypeStruct((m, LANES), F32),
        ],
        compiler_params=_cparams("parallel"),
        name="router",
    )(x, g, rw_pad)


def _moe_kernel(be_ref, nv_ref, x_ref, wg_ref, wu_ref, wd_ref, o_ref, xb_s, *, tm):
    i = pl.program_id(0)
    f = pl.program_id(1)

    @pl.when(f == 0)
    def _():
        o_ref[...] = jnp.zeros_like(o_ref)
        chunk = 64

        def body(j, c):
            r = pl.multiple_of(j * chunk, chunk)
            xb_s[pl.ds(r, chunk), :] = x_ref[pl.ds(r, chunk), :].astype(BF16)
            return c
        lax.fori_loop(0, tm // chunk, body, 0)

    @pl.when(i < nv_ref[0])
    def _():
        o_ref[...] += _swiglu_tile(xb_s[...], wg_ref, wu_ref, wd_ref)


def _moe_experts(xs, block_expert, n_valid, wg, wu, wd, idx, tm, tf=256):
    rows, d = xs.shape
    ff = wg.shape[-1]
    nf = ff // tf

    def f_eff(i, f, nv):
        return jnp.where(i < nv[0], f, nf - 1)

    grid_spec = pltpu.PrefetchScalarGridSpec(
        num_scalar_prefetch=2,
        grid=(rows // tm, nf),
        in_specs=[
            pl.BlockSpec((tm, d), lambda i, f, be, nv: (i, 0)),
            pl.BlockSpec((None, None, d, tf), lambda i, f, be, nv: (idx, be[i], 0, f_eff(i, f, nv))),
            pl.BlockSpec((None, None, d, tf), lambda i, f, be, nv: (idx, be[i], 0, f_eff(i, f, nv))),
            pl.BlockSpec((None, None, tf, d), lambda i, f, be, nv: (idx, be[i], f_eff(i, f, nv), 0)),
        ],
        out_specs=pl.BlockSpec((tm, d), lambda i, f, be, nv: (i, 0)),
        scratch_shapes=[pltpu.VMEM((tm, d), BF16)],
    )
    return pl.pallas_call(
        functools.partial(_moe_kernel, tm=tm),
        grid_spec=grid_spec,
        out_shape=jax.ShapeDtypeStruct((rows, d), F32),
        compiler_params=_cparams("arbitrary", "arbitrary"),
        name="moe_experts",
    )(block_expert, n_valid, xs, wg, wu, wd)


def _moe_combine_kernel(x_ref, y1_ref, y2_ref, info_ref, g_ref, o_ref, *, tm):
    chunk = 64

    def body(i, c):
        r = pl.multiple_of(i * chunk, chunk)
        rows = pl.ds(r, chunk)
        info = info_ref[rows, :]
        y = info[:, 2:3] * y1_ref[rows, :] + info[:, 3:4] * y2_ref[rows, :]
        ms = jnp.mean(y * y, axis=-1, keepdims=True)
        o_ref[rows, :] = x_ref[rows, :] + y * lax.rsqrt(ms + RMS_EPS) * g_ref[...]
        return c
    lax.fori_loop(0, tm // chunk, body, 0)


def _moe_combine(x, y1, y2, info, g, tm=512):
    m, d = x.shape
    tm = _pick(m, tm)
    row = pl.BlockSpec((tm, d), lambda i: (i, 0))
    return pl.pallas_call(
        functools.partial(_moe_combine_kernel, tm=tm),
        grid=(m // tm,),
        in_specs=[row, row, row, pl.BlockSpec((tm, LANES), lambda i: (i, 0)), pl.BlockSpec((1, d), lambda i: (0, 0))],
        out_specs=row,
        out_shape=jax.ShapeDtypeStruct((m, d), F32),
        compiler_params=_cparams("parallel"),
        name="moe_combine",
    )(x, y1, y2, info, g)


def _moe_ffn(x, gi, router_w, wg, wu, wd, idx, go, tm=768):
    m, d = x.shape
    n_experts = router_w.shape[-1]
    rw_pad = jnp.zeros((d, LANES), F32).at[:, :n_experts].set(router_w[idx])
    hn, info = _router(x, gi, rw_pad, n_experts)

    expert = jnp.concatenate([info[:, 0], info[:, 1]]).astype(jnp.int32)
    n_pairs = 2 * m
    order = jnp.argsort(expert, stable=True)
    counts = jnp.sum(expert[:, None] == jnp.arange(n_experts)[None, :], axis=0).astype(jnp.int32)
    padded = (counts + tm - 1) // tm * tm
    pad_end = jnp.cumsum(padded)
    start = jnp.cumsum(counts) - counts
    sorted_e = expert[order]
    dest_sorted = (pad_end - padded)[sorted_e] + jnp.arange(n_pairs, dtype=jnp.int32) - start[sorted_e]
    n_blocks = -(-n_pairs // tm) + n_experts
    rows = n_blocks * tm
    src_token = jnp.zeros((rows,), jnp.int32).at[dest_sorted].set((order % m).astype(jnp.int32))
    dest = jnp.zeros((n_pairs,), jnp.int32).at[order].set(dest_sorted)
    n_valid = (pad_end[-1] // tm).astype(jnp.int32)
    blk_start = jnp.minimum(jnp.arange(n_blocks, dtype=jnp.int32), n_valid - 1) * tm
    block_expert = jnp.sum(blk_start[:, None] >= pad_end[None, :], axis=1).astype(jnp.int32)

    xs = jnp.take(hn, src_token, axis=0)
    ys = _moe_experts(xs, block_expert, n_valid.reshape(1), wg, wu, wd, idx, tm)
    y1 = jnp.take(ys, dest[:m], axis=0)
    y2 = jnp.take(ys, dest[m:], axis=0)
    return _moe_combine(x, y1, y2, info, go)


def kernel(x_prompt, x_sample, cache_win_k, cache_win_v, state_conv, w_in, w_out, rel_bias_table, conv_w, conv_b, conv_ln_g, conv_ln_b, g_attn_out, g_conv_out, g_pre_mix, g_post_mix, g_pre_ffn, g_post_ffn, dense_w_gate, dense_w_up, dense_w_down, router_w, moe_w_gate, moe_w_up, moe_w_down):
    _, seq, d = x_prompt.shape
    batch, n_new, _ = x_sample.shape
    depth, _, win, n_heads, _ = cache_win_k.shape
    attn_w = n_heads * HEAD_DIM
    conv_wd = conv_w.shape[-1]
    in_cols = w_in.shape[-1]
    assert in_cols == 3 * attn_w + 2 * conv_wd and attn_w == conv_wd
    assert win == DILATION_PATTERNS[-1][0] and seq % win == 0 and seq >= win
    a_col, g_col = 3 * attn_w // conv_wd, 3 * attn_w // conv_wd + 1
    keep = min(win, seq)
    n_state = CONV_K - 1

    x = jnp.concatenate([x_prompt.reshape(seq, d), x_sample.reshape(batch * n_new, d)], axis=0)
    bias_p = _prompt_bias(rel_bias_table)
    bias_w, bias_old = _sample_bias(rel_bias_table, n_new, win)
    ck = jnp.transpose(cache_win_k, (0, 1, 3, 4, 2)).reshape(depth, batch, attn_w, win)
    cv = jnp.transpose(cache_win_v, (0, 1, 3, 4, 2)).reshape(depth, batch, attn_w, win)
    hps = 8
    eye = jnp.eye(hps, dtype=F32)
    row = lambda v, l: v[l][None, :]

    kp, vp, up, us = [], [], [], []
    nk_s = nv_s = None
    for l in range(depth):
        proj = _in_proj(x, row(g_pre_mix, l), w_in, l)

        attn_p = _prompt_attn(proj, bias_p, seq, attn_w)
        kp.append(proj[seq - keep:seq, attn_w:2 * attn_w].reshape(1, keep, n_heads, HEAD_DIM))
        vp.append(proj[seq - keep:seq, 2 * attn_w:3 * attn_w].reshape(1, keep, n_heads, HEAD_DIM))
        cw = jnp.zeros((32, conv_wd), F32).at[:CONV_K].set(conv_w[l])
        conv_args = (cw, row(conv_b, l), row(conv_ln_g, l), row(conv_ln_b, l))
        w8 = jnp.repeat(conv_w[l], SUBLANES, axis=0)
        conv_p, tail = _conv_prompt(proj, seq, a_col, g_col, conv_wd, w8, *conv_args[1:])
        up.append(tail[32 - n_state:][None])

        proj_s = proj[seq:].reshape(batch, n_new, in_cols)
        q_s = proj_s[:, :, :attn_w].reshape(batch, n_new, n_heads // hps, hps, HEAD_DIM) * (HEAD_DIM ** -0.5)
        qbd = jnp.einsum('btghd,hk->bghtkd', q_s, eye).reshape(batch, n_heads // hps, hps * n_new, hps * HEAD_DIM)
        qbd = qbd.astype(BF16)
        place = lambda a: jnp.pad(jnp.transpose(a, (0, 2, 1)), ((0, 0), (0, 0), (LANES - n_new, 0)))
        knew_t = place(proj_s[:, :, attn_w:2 * attn_w])
        vnew_t = place(proj_s[:, :, 2 * attn_w:3 * attn_w])
        o_s, nk_s, nv_s = _sample_attn(qbd, knew_t, vnew_t, ck, cv, nk_s, nv_s, bias_w, bias_old, l, n_new, hps)
        attn_s = jnp.transpose(o_s, (0, 2, 1)).reshape(batch * n_new, attn_w)
        conv_s, nu = _conv_sample(proj_s, state_conv, l, a_col, g_col, conv_wd, *conv_args)
        us.append(nu)

        attn = jnp.concatenate([attn_p, attn_s], axis=0)
        conv = jnp.concatenate([conv_p, conv_s.reshape(batch * n_new, conv_wd)], axis=0)
        x = _out_proj(attn, conv, row(g_attn_out, l), row(g_conv_out, l), w_out, l, x, row(g_post_mix, l))

        if l % 2 == 0:
            x = _dense_ffn(x, row(g_pre_ffn, l), dense_w_gate, dense_w_up, dense_w_down, l // 2, row(g_post_ffn, l))
        else:
            x = _moe_ffn(x, row(g_pre_ffn, l), router_w, moe_w_gate, moe_w_up, moe_w_down, l // 2, row(g_post_ffn, l))

    unview = lambda c: jnp.transpose(c.reshape(depth, batch, n_heads, HEAD_DIM, win), (0, 1, 4, 2, 3))
    return (x[:seq].reshape(1, seq, d), x[seq:].reshape(batch, n_new, d),
            jnp.stack(kp), jnp.stack(vp), jnp.stack(up),
            unview(nk_s), unview(nv_s), jnp.stack(us))
```
